```python
import jax, jax.numpy as jnp
from jax import lax
import numpy as np

D_MODEL = 1024
BATCH = 16
SEQ = 2048
DEPTH = 1
DEC_BATCH = 8
DEC_SEQ = 64
PAST_LEN = 2048

CHUNK = 64
QUERY_BLOCK = 128
MIX_WIDTH = D_MODEL
RET_HEADS = 4
RET_HEAD_DIM = MIX_WIDTH // 2 // RET_HEADS
RET_WIDTH = RET_HEADS * RET_HEAD_DIM
RET_ROPE_THETA = 10000.0
ATT_HEADS = 8
ATT_HEAD_DIM = (MIX_WIDTH - RET_WIDTH) // ATT_HEADS
ATT_WIDTH = ATT_HEADS * ATT_HEAD_DIM
ATT_KV_HEADS = 2
ATT_ROT_DIM = ATT_HEAD_DIM // 4
ATT_ROPE_THETA = 500000.0
IDX_HEADS = 8
IDX_DIM = 64
IDX_ROT_DIM = IDX_DIM // 4
MAX_TOPK = 256
NORM_EPS = 1e-6
SPLIT_SIZES = (RET_WIDTH, RET_WIDTH, RET_WIDTH, RET_WIDTH,
               ATT_WIDTH, ATT_KV_HEADS * ATT_HEAD_DIM, ATT_KV_HEADS * ATT_HEAD_DIM, ATT_WIDTH,
               IDX_HEADS * IDX_DIM, IDX_DIM, IDX_HEADS)
D_IN = sum(SPLIT_SIZES)
SPLIT_POINTS = tuple(sum(SPLIT_SIZES[:i + 1]) for i in range(len(SPLIT_SIZES) - 1))

kernel_name = "hybrid_retention_dsa_stream_step"


def rmsnorm(x, g):
    xf = x.astype(jnp.float32)
    r = lax.rsqrt(jnp.mean(xf * xf, axis=-1, keepdims=True) + NORM_EPS)
    return (xf * r).astype(x.dtype) * g


def rope(x, pos, rot_dim, theta):
    half = rot_dim // 2
    inv = jnp.power(theta, -jnp.arange(half, dtype=jnp.float32) / half)
    ang = pos.astype(jnp.float32)[:, None] * inv[None, :]
    cos = jnp.cos(ang)[None, :, None, :].astype(x.dtype)
    sin = jnp.sin(ang)[None, :, None, :].astype(x.dtype)
    x1 = x[..., :half]
    x2 = x[..., half:rot_dim]
    return jnp.concatenate([x1 * cos - x2 * sin, x2 * cos + x1 * sin, x[..., rot_dim:]], axis=-1)


def retention(q, k, v, s0):
    b, t, h, dk = q.shape
    dv = v.shape[-1]
    c = min(t, CHUNK)
    nc = t // c
    log_gamma = jnp.log1p(-jnp.exp2(-5.0 - jnp.arange(h, dtype=jnp.float32)))
    n = jnp.arange(c, dtype=jnp.float32)
    intra = jnp.exp(log_gamma[:, None, None] * jnp.abs(n[:, None] - n[None, :])).astype(q.dtype)
    to_end = jnp.exp(log_gamma[None, :] * (c - 1.0 - n)[:, None]).astype(q.dtype)
    from_start = jnp.exp(log_gamma[None, :] * (n + 1.0)[:, None]).astype(q.dtype)
    chunk_decay = jnp.exp(log_gamma * c)
    qc = q.reshape(b, nc, c, h, dk)
    kc = k.reshape(b, nc, c, h, dk)
    vc = v.reshape(b, nc, c, h, dv)
    scores = jnp.einsum('bcnhd,bcmhd->bchnm', qc, kc) * intra
    out = jnp.einsum('bchnm,bcmhe->bcnhe', scores, vc)
    kv = jnp.einsum('bcmhd,bcmhe->cbhde', kc * to_end[:, :, None], vc).astype(jnp.float32)

    def step(s, kv_c):
        return s * chunk_decay[None, :, None, None] + kv_c, s

    s_final, s_prev = lax.scan(step, s0.astype(jnp.float32), kv)
    out = out + jnp.einsum('bcnhd,cbhde->bcnhe', qc * from_start[:, :, None], s_prev.astype(q.dtype))
    return out.reshape(b, t, h, dv), s_final.astype(q.dtype)


def dsa_attend(q, qi, wi, qpos, k, v, ki, kpos, topk):
    b, nq, _, hd = q.shape
    logits = jax.nn.relu(jnp.einsum('bqhd,bld->bqhl', qi, ki))
    index = jnp.einsum('bqhl,bqh->bql', logits, wi).astype(jnp.float32)
    admissible = (kpos // CHUNK)[None, :] <= (qpos // CHUNK)[:, None]
    index = jnp.where(admissible[None], index, -jnp.inf)
    top_val, sel = lax.top_k(index, topk)
    valid = jnp.isfinite(top_val)
    gather = jax.vmap(lambda rows, idx: rows[idx])
    kg = gather(k, sel)
    vg = gather(v, sel)
    qg = q.reshape(b, nq, ATT_KV_HEADS, ATT_HEADS // ATT_KV_HEADS, hd)
    s = jnp.einsum('bqkgd,bqnkd->bqkgn', qg, kg).astype(jnp.float32) * (hd ** -0.5)
    s = jnp.where(valid[:, :, None, None, :], s, -jnp.inf)
    p = jax.nn.softmax(s, axis=-1).astype(v.dtype)
    o = jnp.einsum('bqkgn,bqnkd->bqkgd', p, vg)
    return o.reshape(b, nq, ATT_WIDTH)


def mixer_layer(x, pos, ret_state, past_k, past_v, past_ki, norm_g, w_in, ret_gn_g, w_out):
    b, t, _ = x.shape
    h = rmsnorm(x, norm_g)
    proj = jnp.einsum('btd,de->bte', h, w_in)
    rq, rk, rv, rg, aq, ak, av, ag, iq, ik, iw = jnp.split(proj, SPLIT_POINTS, axis=-1)

    rq = rope(rq.reshape(b, t, RET_HEADS, RET_HEAD_DIM), pos, RET_HEAD_DIM, RET_ROPE_THETA)
    rk = rope(rk.reshape(b, t, RET_HEADS, RET_HEAD_DIM), pos, RET_HEAD_DIM, RET_ROPE_THETA) * (RET_HEAD_DIM ** -0.5)
    rv = rv.reshape(b, t, RET_HEADS, RET_HEAD_DIM)
    ro, new_ret = retention(rq, rk, rv, ret_state)
    ro = rmsnorm(ro, ret_gn_g.reshape(RET_HEADS, RET_HEAD_DIM)).reshape(b, t, RET_WIDTH) * jax.nn.silu(rg)

    aq = rope(aq.reshape(b, t, ATT_HEADS, ATT_HEAD_DIM), pos, ATT_ROT_DIM, ATT_ROPE_THETA)
    ak = rope(ak.reshape(b, t, ATT_KV_HEADS, ATT_HEAD_DIM), pos, ATT_ROT_DIM, ATT_ROPE_THETA)
    av = av.reshape(b, t, ATT_KV_HEADS, ATT_HEAD_DIM)
    iq = rope(iq.reshape(b, t, IDX_HEADS, IDX_DIM), pos, IDX_ROT_DIM, ATT_ROPE_THETA)
    ik = rope(ik.reshape(b, t, 1, IDX_DIM), pos, IDX_ROT_DIM, ATT_ROPE_THETA)[:, :, 0]
    iw = iw * (IDX_HEADS ** -0.5 * IDX_DIM ** -0.5)

    if past_k is None:
        n_keys = t
        topk = min(MAX_TOPK, n_keys // 4)
        nb = t // QUERY_BLOCK

        def blocks(a):
            return a.reshape((b, nb, QUERY_BLOCK) + a.shape[2:]).swapaxes(0, 1)

        def attend_block(args):
            qb, qib, wib, posb = args
            return dsa_attend(qb, qib, wib, posb, ak, av, ik, pos, topk)

        ao = lax.map(attend_block, (blocks(aq), blocks(iq), blocks(iw), pos.reshape(nb, QUERY_BLOCK)))
        ao = ao.swapaxes(0, 1).reshape(b, t, ATT_WIDTH)
    else:
        n_keys = past_k.shape[1] + t
        topk = min(MAX_TOPK, n_keys // 4)
        k_all = jnp.concatenate([past_k, ak], axis=1)
        v_all = jnp.concatenate([past_v, av], axis=1)
        ki_all = jnp.concatenate([past_ki, ik], axis=1)
        kpos = jnp.arange(n_keys, dtype=jnp.int32)
        ao = dsa_attend(aq, iq, iw, pos, k_all, v_all, ki_all, kpos, topk)
    ao = ao * jax.nn.silu(ag)

    mix = jnp.einsum('bte,ed->btd', jnp.concatenate([ro, ao], axis=-1), w_out)
    return x + mix, new_ret, ak, av, ik


def setup_inputs(seed: int = 0) -> dict:
    key = jax.random.key(seed)
    ks = jax.random.split(key, 11)
    f32 = jnp.float32
    return {
        'x_prompt': jax.random.normal(ks[0], (BATCH, SEQ, D_MODEL), f32),
        'x_sample': jax.random.normal(ks[1], (DEC_BATCH, DEC_SEQ, D_MODEL), f32),
        'state_ret': 0.05 * jax.random.normal(ks[2], (DEPTH, DEC_BATCH, RET_HEADS, RET_HEAD_DIM, RET_HEAD_DIM), f32),
        'cache_k': jax.random.normal(ks[3], (DEPTH, DEC_BATCH, PAST_LEN, ATT_KV_HEADS, ATT_HEAD_DIM), f32),
        'cache_v': jax.random.normal(ks[4], (DEPTH, DEC_BATCH, PAST_LEN, ATT_KV_HEADS, ATT_HEAD_DIM), f32),
        'cache_kidx': jax.random.normal(ks[5], (DEPTH, DEC_BATCH, PAST_LEN, IDX_DIM), f32),
        'norm_g': 1.0 + 0.05 * jax.random.normal(ks[6], (DEPTH, D_MODEL), f32),
        'w_in': jax.random.normal(ks[7], (DEPTH, D_MODEL, D_IN), f32) * (D_MODEL ** -0.5),
        'ret_gn_g': 1.0 + 0.05 * jax.random.normal(ks[8], (DEPTH, RET_WIDTH), f32),
        'w_out': jax.random.normal(ks[9], (DEPTH, MIX_WIDTH, D_MODEL), f32) * (MIX_WIDTH ** -0.5),
        'final_g': 1.0 + 0.05 * jax.random.normal(ks[10], (D_MODEL,), f32),
    }


def reference(x_prompt, x_sample, state_ret, cache_k, cache_v, cache_kidx, norm_g, w_in, ret_gn_g, w_out, final_g):
    pos_p = jnp.arange(x_prompt.shape[1], dtype=jnp.int32)
    pos_s = PAST_LEN + jnp.arange(x_sample.shape[1], dtype=jnp.int32)
    xp = x_prompt
    xs = x_sample
    ret_p, k_p, v_p, ki_p = [], [], [], []
    ret_s, k_s, v_s, ki_s = [], [], [], []
    for l in range(DEPTH):
        zero_state = jnp.zeros((xp.shape[0], RET_HEADS, RET_HEAD_DIM, RET_HEAD_DIM), xp.dtype)
        xp, sp, kp, vp, ip = mixer_layer(xp, pos_p, zero_state, None, None, None,
                                         norm_g[l], w_in[l], ret_gn_g[l], w_out[l])
        xs, ss, kss, vss, iss = mixer_layer(xs, pos_s, state_ret[l], cache_k[l], cache_v[l], cache_kidx[l],
                                            norm_g[l], w_in[l], ret_gn_g[l], w_out[l])
        ret_p.append(sp); k_p.append(kp); v_p.append(vp); ki_p.append(ip)
        ret_s.append(ss); k_s.append(kss); v_s.append(vss); ki_s.append(iss)
    y_prompt = rmsnorm(xp, final_g)
    y_sample = rmsnorm(xs, final_g)
    return (y_prompt, y_sample,
            jnp.stack(ret_p), jnp.stack(k_p), jnp.stack(v_p), jnp.stack(ki_p),
            jnp.stack(ret_s), jnp.stack(k_s), jnp.stack(v_s), jnp.stack(ki_s))
```

```python
import functools

import jax
import jax.numpy as jnp
import numpy as np
from jax import lax
from jax.experimental import pallas as pl
from jax.experimental.pallas import tpu as pltpu

F32 = jnp.float32
BF16 = jnp.bfloat16

D_MODEL = 1024
CHUNK = 64
RET_HEADS = 4
RET_HEAD_DIM = 128
RET_WIDTH = RET_HEADS * RET_HEAD_DIM
RET_ROPE_THETA = 10000.0
ATT_HEADS = 8
ATT_HEAD_DIM = 64
ATT_WIDTH = ATT_HEADS * ATT_HEAD_DIM
ATT_KV_HEADS = 2
ATT_GROUP = ATT_HEADS // ATT_KV_HEADS
KV_WIDTH = ATT_KV_HEADS * ATT_HEAD_DIM
ATT_ROT_DIM = 16
ATT_ROPE_THETA = 500000.0
IDX_HEADS = 8
IDX_DIM = 64
MAX_TOPK = 256
NORM_EPS = 1e-6

LANES = 128
VMEM_LIMIT_BYTES = 56 * 1024 * 1024

COL_RQ = 0
COL_RK = COL_RQ + RET_WIDTH
COL_RV = COL_RK + RET_WIDTH
COL_RG = COL_RV + RET_WIDTH
COL_AQ = COL_RG + RET_WIDTH
COL_AK = COL_AQ + ATT_WIDTH
COL_AV = COL_AK + KV_WIDTH
COL_AG = COL_AV + KV_WIDTH
COL_IQ = COL_AG + ATT_WIDTH
COL_IK = COL_IQ + IDX_HEADS * IDX_DIM
IW_LANE = IDX_DIM + ATT_ROT_DIM
D_PROJ = COL_IK + LANES

INT_MIN = -(2 ** 31)
KEY_NEG_INF = INT_MIN + 0x7FFFFF
KEY_POS_INF = 0x7F800000


def _silu(x):
    return x * (1.0 / (1.0 + jnp.exp(-x)))


def _proj_kernel(x_ref, g_ref, w_ref, cr_ref, sr_ref, ca_ref, s1_ref, s2_ref,
                 rq_ref, rk_ref, rv_ref, rg_ref, aq_ref, k_ref, v_ref, ag_ref, iq_ref,
                 kidx_ref, iwx_ref):
    x = x_ref[...]
    r = lax.rsqrt(jnp.mean(x * x, axis=-1, keepdims=True) + NORM_EPS)
    h = ((x * r) * g_ref[...]).astype(BF16)

    def proj(col, width):
        return jnp.dot(h, w_ref[:, col:col + width], preferred_element_type=F32)

    cr, sr = cr_ref[...], sr_ref[...]
    ca, s1, s2 = ca_ref[...], s1_ref[...], s2_ref[...]

    def rope_ret(y):
        return y * cr + pltpu.roll(y, RET_HEAD_DIM // 2, 1) * sr

    def rope_att(y):
        return y * ca + pltpu.roll(y, LANES - ATT_ROT_DIM // 2, 1) * s1 + pltpu.roll(y, ATT_ROT_DIM // 2, 1) * s2

    for hh in range(RET_HEADS):
        lo = hh * RET_HEAD_DIM
        rq_ref[:, lo:lo + LANES] = rope_ret(proj(COL_RQ + lo, LANES)).astype(rq_ref.dtype)
        rk_ref[:, lo:lo + LANES] = (rope_ret(proj(COL_RK + lo, LANES)) * (RET_HEAD_DIM ** -0.5)).astype(rk_ref.dtype)
    rv_ref[...] = proj(COL_RV, RET_WIDTH).astype(rv_ref.dtype)
    rg_ref[...] = proj(COL_RG, RET_WIDTH).astype(rg_ref.dtype)
    for gi in range(ATT_WIDTH // LANES):
        lo = gi * LANES
        aq_ref[:, lo:lo + LANES] = (rope_att(proj(COL_AQ + lo, LANES)) * (ATT_HEAD_DIM ** -0.5)).astype(aq_ref.dtype)
        iq_ref[:, lo:lo + LANES] = rope_att(proj(COL_IQ + lo, LANES)).astype(iq_ref.dtype)
    k_ref[...] = rope_att(proj(COL_AK, KV_WIDTH))
    v_ref[...] = proj(COL_AV, KV_WIDTH)
    ag_ref[...] = proj(COL_AG, ATT_WIDTH).astype(ag_ref.dtype)
    ikw = rope_att(proj(COL_IK, LANES))
    kidx_ref[...] = ikw[:, :IDX_DIM]
    iwx_ref[...] = ikw * (IDX_HEADS ** -0.5 * IDX_DIM ** -0.5)


def _proj_call(x2d, norm_g, w_all, tabs, seq, tm):
    n = x2d.shape[0]
    nt = n // tm
    tab_blocks = tabs[0].shape[0] // tm

    def row_map(i):
        return (i, 0)

    def tab_map(i):
        return (i % tab_blocks, 0)

    const = lambda i: (0, 0)
    row = lambda w, dt: jax.ShapeDtypeStruct((n, w), dt)
    out_shape = (row(RET_WIDTH, BF16), row(RET_WIDTH, BF16), row(RET_WIDTH, BF16), row(RET_WIDTH, BF16),
                 row(ATT_WIDTH, BF16), row(KV_WIDTH, F32), row(KV_WIDTH, F32), row(ATT_WIDTH, BF16),
                 row(IDX_HEADS * IDX_DIM, BF16), row(IDX_DIM, F32), row(LANES, F32))
    out_specs = tuple(pl.BlockSpec((tm, s.shape[1]), row_map) for s in out_shape)
    in_specs = [pl.BlockSpec((tm, D_MODEL), row_map),
                pl.BlockSpec((1, D_MODEL), const),
                pl.BlockSpec((D_MODEL, D_PROJ), const)] + [pl.BlockSpec((tm, LANES), tab_map)] * 5
    return pl.pallas_call(
        _proj_kernel, grid=(nt,), in_specs=in_specs, out_specs=out_specs, out_shape=out_shape,
        compiler_params=pltpu.CompilerParams(dimension_semantics=("arbitrary",), vmem_limit_bytes=VMEM_LIMIT_BYTES),
        name=f"proj_t{seq}",
    )(x2d, norm_g, w_all, *tabs)


def _ret_kernel(q_ref, k_ref, v_ref, g_ref, s0_ref, dmat_ref, te_ref, fs_ref, cd_ref, gn_ref,
                out_ref, sfin_ref, s_scr):
    j = pl.program_id(1)

    @pl.when(j == 0)
    def _():
        s_scr[...] = s0_ref[0]

    for hh in range(RET_HEADS):
        lo = hh * RET_HEAD_DIM
        q = q_ref[:, lo:lo + LANES]
        k = k_ref[:, lo:lo + LANES]
        v = v_ref[:, lo:lo + LANES]
        s_prev = s_scr[hh]
        sc = lax.dot_general(q, k, (((1,), (1,)), ((), ())), preferred_element_type=F32) * dmat_ref[hh]
        o = jnp.dot(sc.astype(BF16), v, preferred_element_type=F32)
        o = o + jnp.dot((q.astype(F32) * fs_ref[hh]).astype(BF16), s_prev.astype(BF16), preferred_element_type=F32)
        kt = (k.astype(F32) * te_ref[hh]).T.astype(BF16)
        kv = jnp.dot(kt, v, preferred_element_type=F32)
        s_scr[hh] = s_prev * cd_ref[hh] + kv
        r = lax.rsqrt(jnp.mean(o * o, axis=-1, keepdims=True) + NORM_EPS)
        on = (o * r) * gn_ref[:, lo:lo + LANES]
        out_ref[:, lo:lo + LANES] = (on * _silu(g_ref[:, lo:lo + LANES].astype(F32))).astype(out_ref.dtype)

    @pl.when(j == pl.num_programs(1) - 1)
    def _():
        sfin_ref[0] = s_scr[...]


def _ret_call(rq, rk, rv, rg, s0, consts, gn, batch, seq, c):
    dmat, te, fs, cd = consts
    nc = seq // c
    row_map = lambda b, j: (b * nc + j, 0)
    st_map = lambda b, j: (b, 0, 0, 0)
    c3 = lambda b, j: (0, 0, 0)
    rows = pl.BlockSpec((c, RET_WIDTH), row_map)
    st = pl.BlockSpec((1, RET_HEADS, RET_HEAD_DIM, RET_HEAD_DIM), st_map)
    return pl.pallas_call(
        _ret_kernel, grid=(batch, nc),
        in_specs=[rows, rows, rows, rows, st,
                  pl.BlockSpec((RET_HEADS, c, c), c3), pl.BlockSpec((RET_HEADS, c, LANES), c3),
                  pl.BlockSpec((RET_HEADS, c, LANES), c3), pl.BlockSpec((RET_HEADS, 1, LANES), c3),
                  pl.BlockSpec((1, RET_WIDTH), lambda b, j: (0, 0))],
        out_specs=(rows, st),
        out_shape=(jax.ShapeDtypeStruct((batch * seq, RET_WIDTH), BF16),
                   jax.ShapeDtypeStruct((batch, RET_HEADS, RET_HEAD_DIM, RET_HEAD_DIM), F32)),
        scratch_shapes=[pltpu.VMEM((RET_HEADS, RET_HEAD_DIM, RET_HEAD_DIM), F32)],
        compiler_params=pltpu.CompilerParams(dimension_semantics=("arbitrary", "arbitrary"),
                                             vmem_limit_bytes=VMEM_LIMIT_BYTES),
        name=f"retention_t{seq}",
    )(rq, rk, rv, rg, s0, dmat, te, fs, cd, gn)


def _dsa_kernel(*refs, past, seq, qb, kt, topk):
    if past:
        (aq_ref, iq_ref, iwx_ref, ag_ref, kn_ref, vn_ref, kin_ref, kp_ref, vp_ref, kip_ref,
         out_ref, kg_s, vg_s, ki_s, key_s, sc_s) = refs
    else:
        (aq_ref, iq_ref, iwx_ref, ag_ref, kn_ref, vn_ref, kin_ref,
         out_ref, kg_s, vg_s, ki_s, key_s, sc_s) = refs
    j = pl.program_id(1)
    n_keys = past + seq
    nkt_max = key_s.shape[0]
    rows4 = ATT_GROUP * qb

    @pl.when(j == 0)
    def _():
        pad = kg_s.shape[1] - n_keys
        for g in range(ATT_KV_HEADS):
            lo = g * ATT_HEAD_DIM
            if past:
                kg_s[g, 0:past, :] = kp_ref[:, lo:lo + ATT_HEAD_DIM].astype(BF16)
                vg_s[g, 0:past, :] = vp_ref[:, lo:lo + ATT_HEAD_DIM].astype(BF16)
            kg_s[g, past:n_keys, :] = kn_ref[:, lo:lo + ATT_HEAD_DIM].astype(BF16)
            vg_s[g, past:n_keys, :] = vn_ref[:, lo:lo + ATT_HEAD_DIM].astype(BF16)
            if pad:
                kg_s[g, n_keys:, :] = jnp.zeros((pad, ATT_HEAD_DIM), BF16)
                vg_s[g, n_keys:, :] = jnp.zeros((pad, ATT_HEAD_DIM), BF16)
        if past:
            ki_s[0:past, :] = kip_ref[...].astype(BF16)
        ki_s[past:n_keys, :] = kin_ref[...].astype(BF16)
        if pad:
            ki_s[n_keys:, :] = jnp.zeros((pad, IDX_DIM), BF16)

    if seq == qb:
        nk = nkt_max
    else:
        nk = lax.div(past + (j + 1) * qb + (kt - 1), kt)
    q_chunk = lax.shift_right_logical(past + j * qb + lax.broadcasted_iota(jnp.int32, (qb, kt), 0), 6)
    lane_pos = lax.broadcasted_iota(jnp.int32, (qb, kt), 1)
    lane128 = lax.broadcasted_iota(jnp.int32, (qb, LANES), 1)

    iq = iq_ref[...]
    iq_heads = [iq[:, h * IDX_DIM:(h + 1) * IDX_DIM] for h in range(IDX_HEADS)]
    iwx = iwx_ref[...]
    w_cols = [iwx[:, IW_LANE + h:IW_LANE + h + 1] for h in range(IDX_HEADS)]

    def score_tile(t, carry):
        ki_t = ki_s[pl.ds(pl.multiple_of(t * kt, kt), kt), :]
        idx = jnp.zeros((qb, kt), F32)
        for h in range(IDX_HEADS):
            lg = lax.dot_general(iq_heads[h], ki_t, (((1,), (1,)), ((), ())), preferred_element_type=F32)
            idx = idx + jnp.maximum(lg, 0.0) * w_cols[h]
        k_chunk = lax.shift_right_logical(t * kt + lane_pos, 6)
        idx = jnp.where(k_chunk <= q_chunk, idx, -jnp.inf)
        idx = jnp.where(idx == 0.0, 0.0, idx)
        bits = pltpu.bitcast(idx, jnp.int32)
        key_s[t] = bits ^ (lax.shift_right_arithmetic(bits, 31) & 0x7FFFFFFF)
        return carry

    lax.fori_loop(0, nk, score_tile, 0)

    def count(pred):
        def body(t, acc):
            for c in range(kt // LANES):
                key = key_s[t, :, c * LANES:(c + 1) * LANES]
                acc = acc + jnp.where(pred(key, t * kt + c * LANES + lane128), 1.0, 0.0)
            return acc
        acc = lax.fori_loop(0, nk, body, jnp.zeros((qb, LANES), F32))
        return jnp.sum(acc, axis=-1, keepdims=True)

    kf = float(topk)
    thr = jnp.where(count(lambda key, pos: key >= 0) >= kf, 0, INT_MIN).astype(jnp.int32)

    def bit_step(i, thr):
        cand = thr | lax.shift_left(jnp.int32(1), 30 - i)
        c = count(lambda key, pos: key >= cand)
        return jnp.where(c >= kf, cand, thr)

    thr = lax.fori_loop(0, 31, bit_step, thr)
    n_gt = count(lambda key, pos: key > thr)
    n_eq = count(lambda key, pos: key == thr)
    need = kf - n_gt
    split = (n_eq > need) & (thr > KEY_NEG_INF) & (thr < KEY_POS_INF)
    any_split = jnp.max(jnp.where(split, 1.0, 0.0)) > 0.0

    def tie_search(_):
        def pos_step(i, lo):
            cand = lo | lax.shift_left(jnp.int32(1), 11 - i)
            c = count(lambda key, pos: (key == thr) & (pos < cand))
            return jnp.where(c < need, cand, lo)
        return lax.fori_loop(0, 12, pos_step, jnp.zeros((qb, 1), jnp.int32))

    last_eq = lax.cond(any_split, tie_search, lambda _: jnp.full((qb, 1), 2 ** 30, jnp.int32), 0)

    def bias_tile(t, carry):
        key = key_s[t]
        pos = t * kt + lane_pos
        sel = (key > thr) | ((key == thr) & (pos <= last_eq))
        sel = sel & (key > KEY_NEG_INF) & (key < KEY_POS_INF)
        key_s[t] = pltpu.bitcast(jnp.where(sel, 0.0, -jnp.inf), jnp.int32)
        return carry

    lax.fori_loop(0, nk, bias_tile, 0)

    aq = aq_ref[...]
    outs = []
    for g in range(ATT_KV_HEADS):
        qg = jnp.concatenate(
            [aq[:, (g * ATT_GROUP + i) * ATT_HEAD_DIM:(g * ATT_GROUP + i + 1) * ATT_HEAD_DIM] for i in range(ATT_GROUP)],
            axis=0)

        def qk_tile(t, m):
            k_t = kg_s[g, pl.ds(pl.multiple_of(t * kt, kt), kt), :]
            s = lax.dot_general(qg, k_t, (((1,), (1,)), ((), ())), preferred_element_type=F32)
            s = s.reshape(ATT_GROUP, qb, kt) + pltpu.bitcast(key_s[t], F32)[None]
            sc_s[t] = s.reshape(rows4, kt)
            for c in range(kt // LANES):
                m = jnp.maximum(m, s.reshape(rows4, kt)[:, c * LANES:(c + 1) * LANES])
            return m

        m = lax.fori_loop(0, nk, qk_tile, jnp.full((rows4, LANES), -jnp.inf, F32))
        m = jnp.max(m, axis=-1, keepdims=True)
        m = jnp.where(m == -jnp.inf, 0.0, m)

        def pv_tile(t, carry):
            l, acc = carry
            p = jnp.exp(sc_s[t] - m)
            for c in range(kt // LANES):
                l = l + p[:, c * LANES:(c + 1) * LANES]
            v_t = vg_s[g, pl.ds(pl.multiple_of(t * kt, kt), kt), :]
            acc = acc + jnp.dot(p.astype(BF16), v_t, preferred_element_type=F32)
            return l, acc

        l, acc = lax.fori_loop(0, nk, pv_tile,
                               (jnp.zeros((rows4, LANES), F32), jnp.zeros((rows4, ATT_HEAD_DIM), F32)))
        o = acc / jnp.sum(l, axis=-1, keepdims=True)
        outs.extend(o[i * qb:(i + 1) * qb] for i in range(ATT_GROUP))
    ao = jnp.concatenate(outs, axis=1)
    out_ref[...] = (ao * _silu(ag_ref[...].astype(F32))).astype(out_ref.dtype)


def _dsa_call(aq, iq, iwx, ag, k_new, v_new, ki_new, past_kv, batch, seq, qb, kt):
    past = 0 if past_kv is None else past_kv[0].shape[1]
    n_keys = past + seq
    topk = min(MAX_TOPK, n_keys // 4)
    nq = seq // qb
    nkt = pl.cdiv(n_keys, kt)
    l_pad = nkt * kt
    q_map = lambda b, j: (b * nq + j, 0)
    s_map = lambda b, j: (b, 0)
    qrow = lambda w: pl.BlockSpec((qb, w), q_map)
    in_specs = [qrow(ATT_WIDTH), qrow(IDX_HEADS * IDX_DIM), qrow(LANES), qrow(ATT_WIDTH),
                pl.BlockSpec((seq, KV_WIDTH), s_map), pl.BlockSpec((seq, KV_WIDTH), s_map),
                pl.BlockSpec((seq, IDX_DIM), s_map)]
    args = [aq, iq, iwx, ag, k_new, v_new, ki_new]
    if past:
        in_specs += [pl.BlockSpec((None, past, KV_WIDTH), lambda b, j: (b, 0, 0)),
                     pl.BlockSpec((None, past, KV_WIDTH), lambda b, j: (b, 0, 0)),
                     pl.BlockSpec((None, past, IDX_DIM), lambda b, j: (b, 0, 0))]
        args += list(past_kv)
    kern = functools.partial(_dsa_kernel, past=past, seq=seq, qb=qb, kt=kt, topk=topk)
    return pl.pallas_call(
        kern, grid=(batch, nq), in_specs=in_specs, out_specs=qrow(ATT_WIDTH),
        out_shape=jax.ShapeDtypeStruct((batch * seq, ATT_WIDTH), BF16),
        scratch_shapes=[pltpu.VMEM((ATT_KV_HEADS, l_pad, ATT_HEAD_DIM), BF16),
                        pltpu.VMEM((ATT_KV_HEADS, l_pad, ATT_HEAD_DIM), BF16),
                        pltpu.VMEM((l_pad, IDX_DIM), BF16),
                        pltpu.VMEM((nkt, qb, kt), jnp.int32),
                        pltpu.VMEM((nkt, ATT_GROUP * qb, kt), F32)],
        compiler_params=pltpu.CompilerParams(dimension_semantics=("arbitrary", "arbitrary"),
                                             vmem_limit_bytes=VMEM_LIMIT_BYTES),
        name=f"dsa_t{seq}",
    )(*args)


def _out_kernel(x_ref, ro_ref, ao_ref, w_ref, g_ref, y_ref):
    mix = jnp.dot(ro_ref[...], w_ref[0:RET_WIDTH, :], preferred_element_type=F32)
    mix = mix + jnp.dot(ao_ref[...], w_ref[RET_WIDTH:, :], preferred_element_type=F32)
    z = x_ref[...] + mix
    r = lax.rsqrt(jnp.mean(z * z, axis=-1, keepdims=True) + NORM_EPS)
    y_ref[...] = (z * r) * g_ref[...]


def _out_call(x2d, ro, ao, w_out, final_g, seq, tm):
    n = x2d.shape[0]
    row_map = lambda i: (i, 0)
    const = lambda i: (0, 0)
    return pl.pallas_call(
        _out_kernel, grid=(n // tm,),
        in_specs=[pl.BlockSpec((tm, D_MODEL), row_map), pl.BlockSpec((tm, RET_WIDTH), row_map),
                  pl.BlockSpec((tm, ATT_WIDTH), row_map), pl.BlockSpec((D_MODEL, D_MODEL), const),
                  pl.BlockSpec((1, D_MODEL), const)],
        out_specs=pl.BlockSpec((tm, D_MODEL), row_map),
        out_shape=jax.ShapeDtypeStruct((n, D_MODEL), F32),
        compiler_params=pltpu.CompilerParams(dimension_semantics=("arbitrary",), vmem_limit_bytes=VMEM_LIMIT_BYTES),
        name=f"outproj_t{seq}",
    )(x2d, ro, ao, w_out, final_g)


def _rope_tables(pos):
    posf = pos.astype(F32)[:, None]
    half_r = RET_HEAD_DIM // 2
    inv_r = jnp.power(RET_ROPE_THETA, -jnp.arange(half_r, dtype=F32) / half_r)
    ang = posf * inv_r[None, :]
    cos_r = jnp.concatenate([jnp.cos(ang), jnp.cos(ang)], axis=1)
    sin_r = jnp.concatenate([-jnp.sin(ang), jnp.sin(ang)], axis=1)
    half_a = ATT_ROT_DIM // 2
    inv_a = jnp.power(ATT_ROPE_THETA, -jnp.arange(half_a, dtype=F32) / half_a)
    ang = posf * inv_a[None, :]
    n = pos.shape[0]
    rest = ATT_HEAD_DIM - ATT_ROT_DIM
    zeros = lambda w: jnp.zeros((n, w), F32)
    cos_a = jnp.concatenate([jnp.cos(ang), jnp.cos(ang), jnp.ones((n, rest), F32)], axis=1)
    s1 = jnp.concatenate([-jnp.sin(ang), zeros(half_a + rest)], axis=1)
    s2 = jnp.concatenate([zeros(half_a), jnp.sin(ang), zeros(rest)], axis=1)
    two = lambda a: jnp.concatenate([a, a], axis=1)
    return cos_r, sin_r, two(cos_a), two(s1), two(s2)


def _decay_tables(c):
    log_gamma = jnp.log1p(-jnp.exp2(-5.0 - jnp.arange(RET_HEADS, dtype=F32)))[:, None, None]
    n = jnp.arange(c, dtype=F32)
    diff = n[:, None] - n[None, :]
    ch = jnp.arange(c) // CHUNK
    same = ch[:, None] == ch[None, :]
    earlier = ch[None, :] < ch[:, None]
    dmat = jnp.where(same[None], jnp.exp(log_gamma * jnp.abs(diff)[None]),
                     jnp.where(earlier[None], jnp.exp(log_gamma * diff[None]), 0.0))
    bc = lambda a: jnp.broadcast_to(a, (RET_HEADS, a.shape[1], LANES))
    te = bc(jnp.exp(log_gamma * (c - 1.0 - n)[None, :, None]))
    fs = bc(jnp.exp(log_gamma * (n + 1.0)[None, :, None]))
    cd = bc(jnp.exp(log_gamma * c))
    return dmat, te, fs, cd


def _prep_w_in(w_in):
    n_main = COL_IK + IDX_DIM
    d = w_in.shape[0]
    z = lambda w: jnp.zeros((d, w), w_in.dtype)
    w = jnp.concatenate([w_in[:, :n_main], z(ATT_ROT_DIM), w_in[:, n_main:], z(LANES - IW_LANE - IDX_HEADS)], axis=1)
    return w.astype(BF16)


def _layer(x, pos0, s0, past_kv, norm_g, w_all, ret_gn_g, w_out, final_g, *, tm, c, qb, kt):
    batch, seq, _ = x.shape
    x2d = x.reshape(batch * seq, D_MODEL)
    reps = max(tm // seq, 1)
    pos = pos0 + jnp.tile(jnp.arange(seq, dtype=jnp.int32), reps)
    tabs = _rope_tables(pos)
    rq, rk, rv, rg, aq, k_new, v_new, ag, iq, kidx, iwx = _proj_call(x2d, norm_g, w_all, tabs, seq, tm)
    ro, s_fin = _ret_call(rq, rk, rv, rg, s0, _decay_tables(c), ret_gn_g, batch, seq, c)
    ao = _dsa_call(aq, iq, iwx, ag, k_new, v_new, kidx, past_kv, batch, seq, qb, kt)
    y = _out_call(x2d, ro, ao, w_out, final_g, seq, tm)
    return (y.reshape(batch, seq, D_MODEL), s_fin[None],
            k_new.reshape(1, batch, seq, ATT_KV_HEADS, ATT_HEAD_DIM),
            v_new.reshape(1, batch, seq, ATT_KV_HEADS, ATT_HEAD_DIM),
            kidx.reshape(1, batch, seq, IDX_DIM))


def kernel(x_prompt, x_sample, state_ret, cache_k, cache_v, cache_kidx, norm_g, w_in, ret_gn_g, w_out, final_g):
    assert norm_g.shape[0] == 1, "single-layer trunk"
    w_all = _prep_w_in(w_in[0])
    w_o = w_out[0].astype(BF16)
    g_in = norm_g[0][None]
    g_gn = ret_gn_g[0][None]
    g_fin = final_g[None]
    bp, sp, _ = x_prompt.shape
    bs, ss, _ = x_sample.shape
    past = cache_k.shape[2]
    zero_state = jnp.zeros((bp, RET_HEADS, RET_HEAD_DIM, RET_HEAD_DIM), F32)
    yp, sfp, kp, vp, ip = _layer(x_prompt, 0, zero_state, None, g_in, w_all, g_gn, w_o, g_fin,
                                 tm=512, c=256, qb=128, kt=256)
    past_kv = (cache_k[0].reshape(bs, past, KV_WIDTH), cache_v[0].reshape(bs, past, KV_WIDTH), cache_kidx[0])
    ys, sfs, ks, vs, is_ = _layer(x_sample, past, state_ret[0], past_kv, g_in, w_all, g_gn, w_o, g_fin,
                                  tm=bs * ss, c=ss, qb=ss, kt=256)
    return yp, ys, sfp, kp, vp, ip, sfs, ks, vs, is_
```

```python
import functools

import jax
import jax.numpy as jnp
from jax import lax
from jax.experimental import pallas as pl
from jax.experimental.pallas import tpu as pltpu

F32 = jnp.float32
BF16 = jnp.bfloat16

D_MODEL = 1024
CHUNK = 64
RET_HEADS = 4
RET_HEAD_DIM = 128
RET_WIDTH = RET_HEADS * RET_HEAD_DIM
RET_ROPE_THETA = 10000.0
ATT_HEADS = 8
ATT_HEAD_DIM = 64
ATT_WIDTH = ATT_HEADS * ATT_HEAD_DIM
ATT_KV_HEADS = 2
ATT_GROUP = ATT_HEADS // ATT_KV_HEADS
KV_WIDTH = ATT_KV_HEADS * ATT_HEAD_DIM
ATT_ROT_DIM = 16
ATT_ROPE_THETA = 500000.0
IDX_HEADS = 8
IDX_DIM = 64
IDX_WIDTH = IDX_HEADS * IDX_DIM
MAX_TOPK = 256
NORM_EPS = 1e-6

LANES = 128
SUBLANES = 8
VMEM_LIMIT_BYTES = 56 * 1024 * 1024

COL_RQ = 0
COL_RK = COL_RQ + RET_WIDTH
COL_RV = COL_RK + RET_WIDTH
COL_RG = COL_RV + RET_WIDTH
COL_AQ = COL_RG + RET_WIDTH
COL_AK = COL_AQ + ATT_WIDTH
COL_AV = COL_AK + KV_WIDTH
COL_AG = COL_AV + KV_WIDTH
COL_IQ = COL_AG + ATT_WIDTH
COL_IK = COL_IQ + IDX_WIDTH
IW_LANE = IDX_DIM + ATT_ROT_DIM
D_PROJ = COL_IK + LANES

INT_MIN = -(2 ** 31)
KEY_NEG_INF = INT_MIN + 0x7FFFFF
KEY_POS_INF = 0x7F800000
POS_BITS = 12


def _silu(x):
    return x * (1.0 / (1.0 + jnp.exp(-x)))


def _proj_kernel(x_ref, g_ref, w_ref, cr_ref, sr_ref, ca_ref, s1_ref, s2_ref,
                 rq_ref, rk_ref, rv_ref, rg_ref, aqt_ref, k_ref, v_ref, agt_ref, iqt_ref,
                 kidx_ref, iwt_ref, *, qb):
    x = x_ref[...]
    r = lax.rsqrt(jnp.mean(x * x, axis=-1, keepdims=True) + NORM_EPS)
    h = ((x * r) * g_ref[...]).astype(BF16)
    n_qb = x.shape[0] // qb

    def proj(col, width):
        return jnp.dot(h, w_ref[:, col:col + width], preferred_element_type=F32)

    cr, sr = cr_ref[...], sr_ref[...]
    ca, s1, s2 = ca_ref[...], s1_ref[...], s2_ref[...]

    def rope_ret(y):
        return y * cr + pltpu.roll(y, RET_HEAD_DIM // 2, 1) * sr

    def rope_att(y):
        return y * ca + pltpu.roll(y, LANES - ATT_ROT_DIM // 2, 1) * s1 + pltpu.roll(y, ATT_ROT_DIM // 2, 1) * s2

    def put_t(ref, row, y):
        yt = y.T
        for qi in range(n_qb):
            ref[qi, row:row + yt.shape[0], :] = yt[:, qi * qb:(qi + 1) * qb].astype(ref.dtype)

    for hh in range(RET_HEADS):
        lo = hh * RET_HEAD_DIM
        rq_ref[:, lo:lo + LANES] = rope_ret(proj(COL_RQ + lo, LANES)).astype(rq_ref.dtype)
        rk_ref[:, lo:lo + LANES] = (rope_ret(proj(COL_RK + lo, LANES)) * (RET_HEAD_DIM ** -0.5)).astype(rk_ref.dtype)
    rv_ref[...] = proj(COL_RV, RET_WIDTH).astype(rv_ref.dtype)
    rg_ref[...] = _silu(proj(COL_RG, RET_WIDTH)).astype(rg_ref.dtype)
    for gi in range(ATT_WIDTH // LANES):
        lo = gi * LANES
        put_t(aqt_ref, lo, rope_att(proj(COL_AQ + lo, LANES)) * (ATT_HEAD_DIM ** -0.5))
        put_t(iqt_ref, lo, rope_att(proj(COL_IQ + lo, LANES)))
        put_t(agt_ref, lo, _silu(proj(COL_AG + lo, LANES)))
    k_ref[...] = rope_att(proj(COL_AK, KV_WIDTH))
    v_ref[...] = proj(COL_AV, KV_WIDTH)
    ikw = rope_att(proj(COL_IK, LANES))
    kidx_ref[...] = ikw[:, :IDX_DIM]
    put_t(iwt_ref, 0, ikw[:, IW_LANE:IW_LANE + IDX_HEADS] * (IDX_HEADS ** -0.5 * IDX_DIM ** -0.5))


def _proj_call(x2d, norm_g, w_all, tabs, seq, tm, qb):
    n = x2d.shape[0]
    nt = n // tm
    tab_blocks = tabs[0].shape[0] // tm
    n_qb = tm // qb

    row_map = lambda i: (i, 0)
    tab_map = lambda i: (i % tab_blocks, 0)
    const = lambda i: (0, 0)
    t_map = lambda i: (i, 0, 0)
    row = lambda w, dt: (jax.ShapeDtypeStruct((n, w), dt), pl.BlockSpec((tm, w), row_map))
    tr = lambda w, dt: (jax.ShapeDtypeStruct((n // qb, w, qb), dt), pl.BlockSpec((n_qb, w, qb), t_map))
    outs = (row(RET_WIDTH, BF16), row(RET_WIDTH, BF16), row(RET_WIDTH, BF16), row(RET_WIDTH, BF16),
            tr(ATT_WIDTH, BF16), row(KV_WIDTH, F32), row(KV_WIDTH, F32), tr(ATT_WIDTH, BF16),
            tr(IDX_WIDTH, BF16), row(IDX_DIM, F32), tr(IDX_HEADS, F32))
    in_specs = [pl.BlockSpec((tm, D_MODEL), row_map),
                pl.BlockSpec((1, D_MODEL), const),
                pl.BlockSpec((D_MODEL, D_PROJ), const)] + [pl.BlockSpec((tm, LANES), tab_map)] * 5
    return pl.pallas_call(
        functools.partial(_proj_kernel, qb=qb), grid=(nt,), in_specs=in_specs,
        out_specs=tuple(o[1] for o in outs), out_shape=tuple(o[0] for o in outs),
        compiler_params=pltpu.CompilerParams(dimension_semantics=("arbitrary",), vmem_limit_bytes=VMEM_LIMIT_BYTES),
        name=f"proj_t{seq}",
    )(x2d, norm_g, w_all, *tabs)


def _ret_kernel(q_ref, k_ref, v_ref, g_ref, s0_ref, dmat_ref, te_ref, fs_ref, cd_ref, gn_ref,
                out_ref, sfin_ref, s_scr):
    j = pl.program_id(1)

    @pl.when(j == 0)
    def _():
        s_scr[...] = s0_ref[0]

    for hh in range(RET_HEADS):
        lo = hh * RET_HEAD_DIM
        q = q_ref[:, lo:lo + LANES]
        k = k_ref[:, lo:lo + LANES]
        v = v_ref[:, lo:lo + LANES]
        s_prev = s_scr[hh]
        sc = lax.dot_general(q, k, (((1,), (1,)), ((), ())), preferred_element_type=F32) * dmat_ref[hh]
        o = jnp.dot(sc.astype(BF16), v, preferred_element_type=F32)
        o = o + jnp.dot((q.astype(F32) * fs_ref[hh]).astype(BF16), s_prev.astype(BF16), preferred_element_type=F32)
        kt = (k.astype(F32) * te_ref[hh]).T.astype(BF16)
        kv = jnp.dot(kt, v, preferred_element_type=F32)
        s_scr[hh] = s_prev * cd_ref[hh] + kv
        r = lax.rsqrt(jnp.mean(o * o, axis=-1, keepdims=True) + NORM_EPS)
        on = (o * r) * gn_ref[:, lo:lo + LANES]
        out_ref[:, lo:lo + LANES] = (on * g_ref[:, lo:lo + LANES].astype(F32)).astype(out_ref.dtype)

    @pl.when(j == pl.num_programs(1) - 1)
    def _():
        sfin_ref[0] = s_scr[...]


def _ret_call(rq, rk, rv, rg, s0, consts, gn, batch, seq, c):
    dmat, te, fs, cd = consts
    nc = seq // c
    row_map = lambda b, j: (b * nc + j, 0)
    st_map = lambda b, j: (b, 0, 0, 0)
    c3 = lambda b, j: (0, 0, 0)
    rows = pl.BlockSpec((c, RET_WIDTH), row_map)
    st = pl.BlockSpec((1, RET_HEADS, RET_HEAD_DIM, RET_HEAD_DIM), st_map)
    return pl.pallas_call(
        _ret_kernel, grid=(batch, nc),
        in_specs=[rows, rows, rows, rows, st,
                  pl.BlockSpec((RET_HEADS, c, c), c3), pl.BlockSpec((RET_HEADS, c, LANES), c3),
                  pl.BlockSpec((RET_HEADS, c, LANES), c3), pl.BlockSpec((RET_HEADS, 1, LANES), c3),
                  pl.BlockSpec((1, RET_WIDTH), lambda b, j: (0, 0))],
        out_specs=(rows, st),
        out_shape=(jax.ShapeDtypeStruct((batch * seq, RET_WIDTH), BF16),
                   jax.ShapeDtypeStruct((batch, RET_HEADS, RET_HEAD_DIM, RET_HEAD_DIM), F32)),
        scratch_shapes=[pltpu.VMEM((RET_HEADS, RET_HEAD_DIM, RET_HEAD_DIM), F32)],
        compiler_params=pltpu.CompilerParams(dimension_semantics=("arbitrary", "arbitrary"),
                                             vmem_limit_bytes=VMEM_LIMIT_BYTES),
        name=f"retention_t{seq}",
    )(rq, rk, rv, rg, s0, dmat, te, fs, cd, gn)


def _dsa_kernel(*refs, past, seq, qb, kt, topk):
    if past:
        (aqt_ref, iqt_ref, iwt_ref, gt_ref, kn_ref, vn_ref, kin_ref, kp_ref, vp_ref, kip_ref,
         out_ref, kg_s, vt_s, ki_s, key_s, sc_s, acc_s) = refs
    else:
        (aqt_ref, iqt_ref, iwt_ref, gt_ref, kn_ref, vn_ref, kin_ref,
         out_ref, kg_s, vt_s, ki_s, key_s, sc_s, acc_s) = refs
    j = pl.program_id(1)
    n_keys = past + seq
    nkt = key_s.shape[0]
    pad = nkt * kt - n_keys
    gq = ATT_GROUP * qb

    @pl.when(j == 0)
    def _():
        for g in range(ATT_KV_HEADS):
            lo = g * ATT_HEAD_DIM
            if past:
                kg_s[g, 0:past, :] = kp_ref[:, lo:lo + ATT_HEAD_DIM].astype(BF16)
            kg_s[g, past:n_keys, :] = kn_ref[:, lo:lo + ATT_HEAD_DIM].astype(BF16)
            if pad:
                kg_s[g, n_keys:, :] = jnp.zeros((pad, ATT_HEAD_DIM), BF16)
        if past:
            ki_s[0:past, :] = kip_ref[...].astype(BF16)
        ki_s[past:n_keys, :] = kin_ref[...].astype(BF16)
        if pad:
            ki_s[n_keys:, :] = jnp.zeros((pad, IDX_DIM), BF16)
        for t in range(nkt):
            lo = t * kt
            if lo + kt <= past:
                rows = vp_ref[lo:lo + kt, :]
            else:
                cnt = min(kt, n_keys - lo)
                rows = vn_ref[lo - past:lo - past + cnt, :]
                if cnt < kt:
                    rows = jnp.concatenate([rows, jnp.zeros((kt - cnt, KV_WIDTH), F32)], axis=0)
            vt_s[t] = rows.T.astype(BF16)

    if seq == qb:
        nk = nkt
    else:
        nk = lax.div(past + (j + 1) * qb + (kt - 1), kt)
    q_chunk = lax.shift_right_logical(past + j * qb + lax.broadcasted_iota(jnp.int32, (kt, qb), 1), 6)
    sub_pos = lax.broadcasted_iota(jnp.int32, (kt, qb), 0)

    def fold(a, op):
        a = a.reshape(kt // SUBLANES, SUBLANES, a.shape[1])
        while a.shape[0] > 1:
            half = a.shape[0] // 2
            a = op(a[:half], a[half:])
        return a[0]

    iqt = iqt_ref[...]
    iq_all = jnp.concatenate([iqt[h * IDX_DIM:(h + 1) * IDX_DIM, :] for h in range(IDX_HEADS)], axis=1)
    iwt = iwt_ref[...]
    w_all = jnp.concatenate([iwt[h:h + 1, :] for h in range(IDX_HEADS)], axis=1)

    def score_tile(t, carry):
        ki_t = ki_s[pl.ds(pl.multiple_of(t * kt, kt), kt), :]
        lg = jnp.maximum(jnp.dot(ki_t, iq_all, preferred_element_type=F32), 0.0) * w_all
        idx = lg[:, 0:qb]
        for h in range(1, IDX_HEADS):
            idx = idx + lg[:, h * qb:(h + 1) * qb]
        k_chunk = lax.shift_right_logical(t * kt + sub_pos, 6)
        idx = jnp.where(k_chunk <= q_chunk, idx, -jnp.inf)
        idx = jnp.where(idx == 0.0, 0.0, idx)
        bits = pltpu.bitcast(idx, jnp.int32)
        key_s[t] = bits ^ (lax.shift_right_arithmetic(bits, 31) & 0x7FFFFFFF)
        return carry

    lax.fori_loop(0, nk, score_tile, 0)

    def count(pred):
        def body(t, acc):
            return acc + fold(jnp.where(pred(key_s[t], t * kt + sub_pos), 1.0, 0.0), jnp.add)
        acc = lax.fori_loop(0, nk, body, jnp.zeros((SUBLANES, qb), F32))
        return jnp.sum(acc, axis=0, keepdims=True)

    kf = float(topk)
    thr = jnp.where(count(lambda key, pos: key >= 0) >= kf, 0, INT_MIN).astype(jnp.int32)

    def bit_step(i, thr):
        cand = thr | lax.shift_left(jnp.int32(1), 30 - i)
        c = count(lambda key, pos: key >= cand)
        return jnp.where(c >= kf, cand, thr)

    thr = lax.fori_loop(0, 31, bit_step, thr)
    n_gt = count(lambda key, pos: key > thr)
    n_eq = count(lambda key, pos: key == thr)
    need = kf - n_gt
    split = (n_eq > need) & (thr > KEY_NEG_INF) & (thr < KEY_POS_INF)
    any_split = jnp.max(jnp.where(split, 1.0, 0.0)) > 0.0

    def tie_search(_):
        def pos_step(i, lo):
            cand = lo | lax.shift_left(jnp.int32(1), POS_BITS - 1 - i)
            c = count(lambda key, pos: (key == thr) & (pos < cand))
            return jnp.where(c < need, cand, lo)
        return lax.fori_loop(0, POS_BITS, pos_step, jnp.zeros((1, qb), jnp.int32))

    last_eq = lax.cond(any_split, tie_search, lambda _: jnp.full((1, qb), 2 ** 30, jnp.int32), 0)

    aqt = aqt_ref[...]
    q_all = jnp.concatenate([aqt[h * ATT_HEAD_DIM:(h + 1) * ATT_HEAD_DIM, :] for h in range(ATT_HEADS)], axis=1)

    def qk_tile(t, m_acc):
        key = key_s[t]
        sel = (key > thr) | ((key == thr) & (t * kt + sub_pos <= last_eq))
        sel = sel & (key > KEY_NEG_INF) & (key < KEY_POS_INF)
        bias = jnp.concatenate([jnp.where(sel, 0.0, -jnp.inf)] * ATT_GROUP, axis=1)
        tops = []
        for g in range(ATT_KV_HEADS):
            k_t = kg_s[g, pl.ds(pl.multiple_of(t * kt, kt), kt), :]
            s = jnp.dot(k_t, q_all[:, g * gq:(g + 1) * gq], preferred_element_type=F32) + bias
            sc_s[t, :, g * gq:(g + 1) * gq] = s
            tops.append(fold(s, jnp.maximum))
        return jnp.maximum(m_acc, jnp.concatenate(tops, axis=1))

    m_acc = lax.fori_loop(0, nk, qk_tile, jnp.full((SUBLANES, ATT_HEADS * qb), -jnp.inf, F32))
    m = jnp.max(m_acc, axis=0, keepdims=True)
    m = jnp.where(m == -jnp.inf, 0.0, m)
    acc_s[...] = jnp.zeros(acc_s.shape, F32)

    def pv_tile(t, l_acc):
        p = jnp.exp(sc_s[t] - m)
        for g in range(ATT_KV_HEADS):
            v_t = vt_s[t, g * ATT_HEAD_DIM:(g + 1) * ATT_HEAD_DIM, :]
            acc_s[g] += jnp.dot(v_t, p[:, g * gq:(g + 1) * gq].astype(BF16), preferred_element_type=F32)
        return l_acc + fold(p, jnp.add)

    l_acc = lax.fori_loop(0, nk, pv_tile, jnp.zeros((SUBLANES, ATT_HEADS * qb), F32))
    inv_l = 1.0 / jnp.sum(l_acc, axis=0, keepdims=True)
    for h in range(ATT_HEADS):
        g, i = divmod(h, ATT_GROUP)
        rows = slice(h * ATT_HEAD_DIM, (h + 1) * ATT_HEAD_DIM)
        o_t = acc_s[g, :, i * qb:(i + 1) * qb] * inv_l[:, h * qb:(h + 1) * qb]
        out_ref[rows, :] = (o_t * gt_ref[rows, :].astype(F32)).astype(out_ref.dtype)


def _dsa_call(aqt, iqt, iwt, gt, k_new, v_new, ki_new, past_kv, batch, seq, qb, kt):
    past = 0 if past_kv is None else past_kv[0].shape[1]
    assert past % kt == 0 and seq % qb == 0
    n_keys = past + seq
    assert n_keys < 2 ** POS_BITS
    topk = min(MAX_TOPK, n_keys // 4)
    nq = seq // qb
    nkt = pl.cdiv(n_keys, kt)
    l_pad = nkt * kt
    s_map = lambda b, j: (b, 0)
    tq = lambda w: pl.BlockSpec((None, w, qb), lambda b, j: (b * nq + j, 0, 0))
    in_specs = [tq(ATT_WIDTH), tq(IDX_WIDTH), tq(IDX_HEADS), tq(ATT_WIDTH),
                pl.BlockSpec((seq, KV_WIDTH), s_map), pl.BlockSpec((seq, KV_WIDTH), s_map),
                pl.BlockSpec((seq, IDX_DIM), s_map)]
    args = [aqt, iqt, iwt, gt, k_new, v_new, ki_new]
    if past:
        in_specs += [pl.BlockSpec((None, past, KV_WIDTH), lambda b, j: (b, 0, 0)),
                     pl.BlockSpec((None, past, KV_WIDTH), lambda b, j: (b, 0, 0)),
                     pl.BlockSpec((None, past, IDX_DIM), lambda b, j: (b, 0, 0))]
        args += list(past_kv)
    kern = functools.partial(_dsa_kernel, past=past, seq=seq, qb=qb, kt=kt, topk=topk)
    return pl.pallas_call(
        kern, grid=(batch, nq), in_specs=in_specs, out_specs=tq(ATT_WIDTH),
        out_shape=jax.ShapeDtypeStruct((batch * nq, ATT_WIDTH, qb), BF16),
        scratch_shapes=[pltpu.VMEM((ATT_KV_HEADS, l_pad, ATT_HEAD_DIM), BF16),
                        pltpu.VMEM((nkt, KV_WIDTH, kt), BF16),
                        pltpu.VMEM((l_pad, IDX_DIM), BF16),
                        pltpu.VMEM((nkt, kt, qb), jnp.int32),
                        pltpu.VMEM((nkt, kt, ATT_HEADS * qb), F32),
                        pltpu.VMEM((ATT_KV_HEADS, ATT_HEAD_DIM, ATT_GROUP * qb), F32)],
        compiler_params=pltpu.CompilerParams(dimension_semantics=("arbitrary", "arbitrary"),
                                             vmem_limit_bytes=VMEM_LIMIT_BYTES),
        name=f"dsa_t{seq}",
    )(*args)


def _out_kernel(x_ref, ro_ref, aot_ref, w_ref, g_ref, y_ref):
    aot = jnp.concatenate([aot_ref[qi] for qi in range(aot_ref.shape[0])], axis=1)
    ao = aot.astype(F32).T.astype(BF16)
    mix = jnp.dot(ro_ref[...], w_ref[0:RET_WIDTH, :], preferred_element_type=F32)
    mix = mix + jnp.dot(ao, w_ref[RET_WIDTH:, :], preferred_element_type=F32)
    z = x_ref[...] + mix
    r = lax.rsqrt(jnp.mean(z * z, axis=-1, keepdims=True) + NORM_EPS)
    y_ref[...] = (z * r) * g_ref[...]


def _out_call(x2d, ro, aot, w_out, final_g, seq, tm):
    n = x2d.shape[0]
    qb = aot.shape[2]
    row_map = lambda i: (i, 0)
    const = lambda i: (0, 0)
    return pl.pallas_call(
        _out_kernel, grid=(n // tm,),
        in_specs=[pl.BlockSpec((tm, D_MODEL), row_map), pl.BlockSpec((tm, RET_WIDTH), row_map),
                  pl.BlockSpec((tm // qb, ATT_WIDTH, qb), lambda i: (i, 0, 0)),
                  pl.BlockSpec((D_MODEL, D_MODEL), const), pl.BlockSpec((1, D_MODEL), const)],
        out_specs=pl.BlockSpec((tm, D_MODEL), row_map),
        out_shape=jax.ShapeDtypeStruct((n, D_MODEL), F32),
        compiler_params=pltpu.CompilerParams(dimension_semantics=("arbitrary",), vmem_limit_bytes=VMEM_LIMIT_BYTES),
        name=f"outproj_t{seq}",
    )(x2d, ro, aot, w_out, final_g)


def _rope_tables(pos):
    posf = pos.astype(F32)[:, None]
    half_r = RET_HEAD_DIM // 2
    inv_r = jnp.power(RET_ROPE_THETA, -jnp.arange(half_r, dtype=F32) / half_r)
    ang = posf * inv_r[None, :]
    cos_r = jnp.concatenate([jnp.cos(ang), jnp.cos(ang)], axis=1)
    sin_r = jnp.concatenate([-jnp.sin(ang), jnp.sin(ang)], axis=1)
    half_a = ATT_ROT_DIM // 2
    inv_a = jnp.power(ATT_ROPE_THETA, -jnp.arange(half_a, dtype=F32) / half_a)
    ang = posf * inv_a[None, :]
    n = pos.shape[0]
    rest = ATT_HEAD_DIM - ATT_ROT_DIM
    zeros = lambda w: jnp.zeros((n, w), F32)
    cos_a = jnp.concatenate([jnp.cos(ang), jnp.cos(ang), jnp.ones((n, rest), F32)], axis=1)
    s1 = jnp.concatenate([-jnp.sin(ang), zeros(half_a + rest)], axis=1)
    s2 = jnp.concatenate([zeros(half_a), jnp.sin(ang), zeros(rest)], axis=1)
    two = lambda a: jnp.concatenate([a, a], axis=1)
    return cos_r, sin_r, two(cos_a), two(s1), two(s2)


def _decay_tables(c):
    log_gamma = jnp.log1p(-jnp.exp2(-5.0 - jnp.arange(RET_HEADS, dtype=F32)))[:, None, None]
    n = jnp.arange(c, dtype=F32)
    diff = n[:, None] - n[None, :]
    ch = jnp.arange(c) // CHUNK
    same = ch[:, None] == ch[None, :]
    earlier = ch[None, :] < ch[:, None]
    dmat = jnp.where(same[None], jnp.exp(log_gamma * jnp.abs(diff)[None]),
                     jnp.where(earlier[None], jnp.exp(log_gamma * diff[None]), 0.0))
    bc = lambda a: jnp.broadcast_to(a, (RET_HEADS, a.shape[1], LANES))
    te = bc(jnp.exp(log_gamma * (c - 1.0 - n)[None, :, None]))
    fs = bc(jnp.exp(log_gamma * (n + 1.0)[None, :, None]))
    cd = bc(jnp.exp(log_gamma * c))
    return dmat, te, fs, cd


def _prep_w_in(w_in):
    n_main = COL_IK + IDX_DIM
    d = w_in.shape[0]
    z = lambda w: jnp.zeros((d, w), w_in.dtype)
    w = jnp.concatenate([w_in[:, :n_main], z(ATT_ROT_DIM), w_in[:, n_main:], z(LANES - IW_LANE - IDX_HEADS)], axis=1)
    return w.astype(BF16)


def _layer(x, pos0, s0, past_kv, norm_g, w_all, ret_gn_g, w_out, final_g, *, tm, c, qb, kt):
    batch, seq, _ = x.shape
    x2d = x.reshape(batch * seq, D_MODEL)
    reps = max(tm // seq, 1)
    pos = pos0 + jnp.tile(jnp.arange(seq, dtype=jnp.int32), reps)
    tabs = _rope_tables(pos)
    rq, rk, rv, rg, aqt, k_new, v_new, agt, iqt, kidx, iwt = _proj_call(x2d, norm_g, w_all, tabs, seq, tm, qb)
    ro, s_fin = _ret_call(rq, rk, rv, rg, s0, _decay_tables(c), ret_gn_g, batch, seq, c)
    aot = _dsa_call(aqt, iqt, iwt, agt, k_new, v_new, kidx, past_kv, batch, seq, qb, kt)
    y = _out_call(x2d, ro, aot, w_out, final_g, seq, tm)
    return (y.reshape(batch, seq, D_MODEL), s_fin[None],
            k_new.reshape(1, batch, seq, ATT_KV_HEADS, ATT_HEAD_DIM),
            v_new.reshape(1, batch, seq, ATT_KV_HEADS, ATT_HEAD_DIM),
            kidx.reshape(1, batch, seq, IDX_DIM))


def kernel(x_prompt, x_sample, state_ret, cache_k, cache_v, cache_kidx, norm_g, w_in, ret_gn_g, w_out, final_g):
    assert norm_g.shape[0] == 1, "single-layer trunk"
    w_all = _prep_w_in(w_in[0])
    w_o = w_out[0].astype(BF16)
    g_in = norm_g[0][None]
    g_gn = ret_gn_g[0][None]
    g_fin = final_g[None]
    bp, sp, _ = x_prompt.shape
    bs, ss, _ = x_sample.shape
    past = cache_k.shape[2]
    zero_state = jnp.zeros((bp, RET_HEADS, RET_HEAD_DIM, RET_HEAD_DIM), F32)
    yp, sfp, kp, vp, ip = _layer(x_prompt, 0, zero_state, None, g_in, w_all, g_gn, w_o, g_fin,
                                 tm=512, c=256, qb=256, kt=256)
    past_kv = (cache_k[0].reshape(bs, past, KV_WIDTH), cache_v[0].reshape(bs, past, KV_WIDTH), cache_kidx[0])
    ys, sfs, ks, vs, is_ = _layer(x_sample, past, state_ret[0], past_kv, g_in, w_all, g_gn, w_o, g_fin,
                                  tm=bs * ss, c=ss, qb=ss, kt=256)
    return yp, ys, sfp, kp, vp, ip, sfs, ks, vs, is_
```
